```python
import jax, jax.numpy as jnp
from jax import lax
import numpy as np

D_MODEL = 2048
BATCH = 4
SEQ = 4096
DEPTH = 2

HEAD_DIM = 128
N_HEADS_MOBA = 8
N_HEADS_FOX = 8
N_HEADS_MLA = 8
MOBA_BLOCK = 256
MOBA_TOPK = 3
MOBA_Q_CHUNK = 64
Q_BLOCK = 128
MLA_Q_LORA = 512
MLA_KV_LORA = 512
MLA_NOPE = 128
MLA_ROPE = 64
MLA_V = 128
ROPE_BASE = 10000.0
N_BRANCH = 3
D_FF = 5632
N_EXPERTS = 8
TOP_K = 2
D_FF_EXPERT = 7168
MOE_ROW_BLOCK = 512
LN_EPS = 1e-5
RMS_EPS = 1e-6
DEEPNORM_ALPHA = (2 * DEPTH) ** 0.25
DEEPNORM_BETA = (8 * DEPTH) ** -0.25

W_MOBA = N_HEADS_MOBA * HEAD_DIM
W_FOX = N_HEADS_FOX * HEAD_DIM
MLA_QK = MLA_NOPE + MLA_ROPE
W_MLA_OUT = N_HEADS_MLA * MLA_V
SPLITS = (3 * W_MOBA, 3 * W_FOX, N_HEADS_FOX, MLA_Q_LORA, MLA_KV_LORA, MLA_ROPE, N_BRANCH * D_MODEL)
D_IN_PROJ = sum(SPLITS)

kernel_name = "hybrid_moba_fox_mla_gated_deepnorm_moe"


def layer_norm(x, g, b):
    xf = x.astype(jnp.float32)
    mu = jnp.mean(xf, -1, keepdims=True)
    var = jnp.mean(jnp.square(xf - mu), -1, keepdims=True)
    return ((xf - mu) * lax.rsqrt(var + LN_EPS)).astype(x.dtype) * g + b


def rms_norm(x, g):
    xf = x.astype(jnp.float32)
    return (xf * lax.rsqrt(jnp.mean(xf * xf, -1, keepdims=True) + RMS_EPS)).astype(x.dtype) * g


def rope_tables(positions, dtype):
    inv = ROPE_BASE ** (-jnp.arange(0, MLA_ROPE, 2, dtype=jnp.float32) / MLA_ROPE)
    ang = positions.astype(jnp.float32)[..., None] * inv
    return jnp.cos(ang).astype(dtype), jnp.sin(ang).astype(dtype)


def apply_rope(t, cos, sin):
    t1, t2 = jnp.split(t, 2, axis=-1)
    return jnp.concatenate([t1 * cos - t2 * sin, t2 * cos + t1 * sin], axis=-1)


def alibi_slopes(n):
    return 2.0 ** (-8.0 * jnp.arange(1, n + 1, dtype=jnp.float32) / n)


def moba_attention(q, k, v):
    b, h, s, d = q.shape
    nb = -(-s // MOBA_BLOCK)
    pad = nb * MOBA_BLOCK - s
    kp = jnp.pad(k, ((0, 0), (0, 0), (0, pad), (0, 0)))
    vp = jnp.pad(v, ((0, 0), (0, 0), (0, pad), (0, 0)))
    kb = kp.reshape(b, h, nb, MOBA_BLOCK, d)
    vb = vp.reshape(b, h, nb, MOBA_BLOCK, d)
    scale = d ** -0.5
    slopes = alibi_slopes(h)[None, :, None, None]
    kmean = jnp.mean(kb.astype(jnp.float32), axis=3)
    gate = jnp.einsum('bhsd,bhnd->bhsn', q.astype(jnp.float32), kmean)
    cur = jnp.arange(s) // MOBA_BLOCK
    past = jnp.arange(nb)[None, :] < cur[:, None]
    gate = jnp.where(past, gate, -jnp.inf)
    k_sel = min(MOBA_TOPK, nb)
    _, sel = lax.top_k(gate, k_sel)
    valid = jnp.arange(k_sel)[None, :] < cur[:, None]
    gather_blocks = jax.vmap(jax.vmap(lambda blk, i: blk[i]))

    def chunk(ci):
        q0 = ci * MOBA_Q_CHUNK
        qc = lax.dynamic_slice_in_dim(q, q0, MOBA_Q_CHUNK, axis=2)
        sc = lax.dynamic_slice_in_dim(sel, q0, MOBA_Q_CHUNK, axis=2)
        vc = lax.dynamic_slice_in_dim(valid, q0, MOBA_Q_CHUNK, axis=0)
        qpos = q0 + jnp.arange(MOBA_Q_CHUNK)
        k_g = gather_blocks(kb, sc)
        v_g = gather_blocks(vb, sc)
        kpos_g = sc[..., None] * MOBA_BLOCK + jnp.arange(MOBA_BLOCK)
        s_g = (jnp.einsum('bhcd,bhckjd->bhckj', qc, k_g).astype(jnp.float32) * scale
               - slopes[..., None] * (qpos[:, None, None] - kpos_g).astype(jnp.float32))
        s_g = jnp.where(vc[:, :, None], s_g, -jnp.inf)
        o0 = (q0 // MOBA_BLOCK) * MOBA_BLOCK
        k_o = lax.dynamic_slice_in_dim(kp, o0, MOBA_BLOCK, axis=2)
        v_o = lax.dynamic_slice_in_dim(vp, o0, MOBA_BLOCK, axis=2)
        kpos_o = o0 + jnp.arange(MOBA_BLOCK)
        dist_o = (qpos[:, None] - kpos_o[None, :]).astype(jnp.float32)
        s_o = jnp.einsum('bhcd,bhjd->bhcj', qc, k_o).astype(jnp.float32) * scale - slopes * dist_o
        s_o = jnp.where(dist_o >= 0, s_o, -jnp.inf)
        logits = jnp.concatenate([s_g.reshape(b, h, MOBA_Q_CHUNK, k_sel * MOBA_BLOCK), s_o], axis=-1)
        p = jax.nn.softmax(logits, axis=-1).astype(v.dtype)
        p_g = p[..., :k_sel * MOBA_BLOCK].reshape(b, h, MOBA_Q_CHUNK, k_sel, MOBA_BLOCK)
        p_o = p[..., k_sel * MOBA_BLOCK:]
        return (jnp.einsum('bhckj,bhckjd->bhcd', p_g, v_g)
                + jnp.einsum('bhcj,bhjd->bhcd', p_o, v_o))

    out = lax.map(chunk, jnp.arange(s // MOBA_Q_CHUNK))
    return out.transpose(1, 2, 0, 3, 4).reshape(b, h, s, d)


def blocked_causal_attention(q, k, v, scale, decay_cum=None):
    b, h, s, _ = q.shape
    dv = v.shape[-1]
    kpos = jnp.arange(s)

    def block(qi):
        q0 = qi * Q_BLOCK
        qb = lax.dynamic_slice_in_dim(q, q0, Q_BLOCK, axis=2)
        qpos = q0 + jnp.arange(Q_BLOCK)
        logits = jnp.einsum('bhqd,bhkd->bhqk', qb, k).astype(jnp.float32) * scale
        if decay_cum is not None:
            cq = lax.dynamic_slice_in_dim(decay_cum, q0, Q_BLOCK, axis=2)
            logits = logits + (cq[..., :, None] - decay_cum[..., None, :])
        logits = jnp.where(kpos[None, :] <= qpos[:, None], logits, -jnp.inf)
        p = jax.nn.softmax(logits, axis=-1).astype(v.dtype)
        return jnp.einsum('bhqk,bhkd->bhqd', p, v)

    out = lax.map(block, jnp.arange(s // Q_BLOCK))
    return out.transpose(1, 2, 0, 3, 4).reshape(b, h, s, dv)


def token_mixer(x, cos, sin, w_in, b_forget, g_q_a, g_kv_a, w_q_b, w_kv_b,
                w_branch_moba, w_branch_fox, w_branch_mla, w_o):
    b, s, _ = x.shape
    proj = x @ w_in
    p_moba, p_fox, p_forget, c_q, c_kv, k_rope_raw, p_gate = jnp.split(
        proj, np.cumsum(SPLITS)[:-1].tolist(), axis=-1)

    def heads(t, n):
        return t.reshape(b, s, n, -1).transpose(0, 2, 1, 3)

    def merge(t):
        return t.transpose(0, 2, 1, 3).reshape(b, s, -1)

    qa, ka, va = [heads(t, N_HEADS_MOBA) for t in jnp.split(p_moba, 3, axis=-1)]
    o_moba = merge(moba_attention(qa, ka, va))

    qf, kf, vf = [heads(t, N_HEADS_FOX) for t in jnp.split(p_fox, 3, axis=-1)]
    log_f = jax.nn.log_sigmoid(p_forget.astype(jnp.float32) + b_forget.astype(jnp.float32))
    c = jnp.cumsum(log_f, axis=1).transpose(0, 2, 1)
    o_fox = merge(blocked_causal_attention(qf, kf, vf, HEAD_DIM ** -0.5, decay_cum=c))

    q_m = (rms_norm(c_q, g_q_a) @ w_q_b).reshape(b, s, N_HEADS_MLA, MLA_QK)
    q_nope, q_pe = jnp.split(q_m, [MLA_NOPE], axis=-1)
    q_pe = apply_rope(q_pe, cos[:, :, None, :], sin[:, :, None, :])
    q_m = jnp.concatenate([q_nope, q_pe], axis=-1)
    kv = (rms_norm(c_kv, g_kv_a) @ w_kv_b).reshape(b, s, N_HEADS_MLA, MLA_NOPE + MLA_V)
    k_nope, v_m = jnp.split(kv, [MLA_NOPE], axis=-1)
    k_pe = apply_rope(k_rope_raw, cos, sin)
    k_m = jnp.concatenate(
        [k_nope, jnp.broadcast_to(k_pe[:, :, None, :], (b, s, N_HEADS_MLA, MLA_ROPE))], axis=-1)
    o_mla = merge(blocked_causal_attention(q_m.transpose(0, 2, 1, 3), k_m.transpose(0, 2, 1, 3),
                                           v_m.transpose(0, 2, 1, 3), MLA_QK ** -0.5))

    g_a, g_b, g_c = jnp.split(p_gate, 3, axis=-1)
    merged = (jax.nn.sigmoid(g_a) * (o_moba @ w_branch_moba)
              + jax.nn.sigmoid(g_b) * (o_fox @ w_branch_fox)
              + jax.nn.sigmoid(g_c) * (o_mla @ w_branch_mla))
    return merged @ w_o


def swiglu(x, w1, w3, w2):
    return (jax.nn.silu(x @ w1) * (x @ w3)) @ w2


def moe_swiglu(x, w_router, w1, w3, w2):
    b, s, d = x.shape
    t = b * s
    xt = x.reshape(t, d)
    logits = (xt @ w_router).astype(jnp.float32)
    top_val, top_idx = lax.top_k(logits, TOP_K)
    gates = jax.nn.softmax(top_val, axis=-1)
    flat_e = top_idx.reshape(-1)
    flat_tok = jnp.repeat(jnp.arange(t, dtype=jnp.int32), TOP_K)
    flat_g = gates.reshape(-1)
    order = jnp.argsort(flat_e)
    e_sorted = flat_e[order]
    counts = jnp.zeros((N_EXPERTS,), jnp.int32).at[flat_e].add(1)
    padded = (counts + MOE_ROW_BLOCK - 1) // MOE_ROW_BLOCK * MOE_ROW_BLOCK
    start_sorted = jnp.cumsum(counts) - counts
    end_padded = jnp.cumsum(padded)
    start_padded = end_padded - padded
    rank = jnp.arange(t * TOP_K, dtype=jnp.int32) - start_sorted[e_sorted]
    dest = start_padded[e_sorted] + rank
    n_rows = (-(-(t * TOP_K) // MOE_ROW_BLOCK) + N_EXPERTS) * MOE_ROW_BLOCK
    row_tok = jnp.full((n_rows,), t, jnp.int32).at[dest].set(flat_tok[order])
    row_gate = jnp.zeros((n_rows,), jnp.float32).at[dest].set(flat_g[order])
    n_blk = n_rows // MOE_ROW_BLOCK
    blk_e = jnp.minimum(
        jnp.searchsorted(end_padded, jnp.arange(n_blk, dtype=jnp.int32) * MOE_ROW_BLOCK, side='right'),
        N_EXPERTS - 1)
    x_pad = jnp.concatenate([xt, jnp.zeros((1, d), xt.dtype)], axis=0)
    xb = x_pad[row_tok].reshape(n_blk, MOE_ROW_BLOCK, d)

    def expert_block(args):
        xi, e = args
        return swiglu(xi, w1[e], w3[e], w2[e])

    yb = lax.map(expert_block, (xb, blk_e)).reshape(n_rows, d)
    y = jnp.zeros((t + 1, d), x.dtype).at[row_tok].add(yb * row_gate[:, None].astype(x.dtype))[:t]
    return y.reshape(b, s, d)


def setup_inputs(seed: int = 0) -> dict:
    key = jax.random.key(seed)
    ks = jax.random.split(key, 24)
    f32 = jnp.float32
    n_dense = (DEPTH + 1) // 2
    n_moe = DEPTH // 2

    def nrm(k, shape, fan_in, mult=1.0):
        return jax.random.normal(k, shape, f32) * (mult * fan_in ** -0.5)

    def gain(k, shape):
        return 1.0 + 0.02 * jax.random.normal(k, shape, f32)

    return {
        "x": jax.random.normal(ks[0], (BATCH, SEQ, D_MODEL), f32),
        "positions": jnp.broadcast_to(jnp.arange(SEQ, dtype=jnp.int32), (BATCH, SEQ)),
        "w_in": nrm(ks[1], (DEPTH, D_MODEL, D_IN_PROJ), D_MODEL),
        "b_forget": jax.random.uniform(ks[2], (DEPTH, N_HEADS_FOX), f32, 1.0, 3.0),
        "g_q_a": gain(ks[3], (DEPTH, MLA_Q_LORA)),
        "g_kv_a": gain(ks[4], (DEPTH, MLA_KV_LORA)),
        "w_q_b": nrm(ks[5], (DEPTH, MLA_Q_LORA, N_HEADS_MLA * MLA_QK), MLA_Q_LORA),
        "w_kv_b": nrm(ks[6], (DEPTH, MLA_KV_LORA, N_HEADS_MLA * (MLA_NOPE + MLA_V)), MLA_KV_LORA),
        "w_branch_moba": nrm(ks[7], (DEPTH, W_MOBA, D_MODEL), W_MOBA),
        "w_branch_fox": nrm(ks[8], (DEPTH, W_FOX, D_MODEL), W_FOX),
        "w_branch_mla": nrm(ks[9], (DEPTH, W_MLA_OUT, D_MODEL), W_MLA_OUT),
        "w_o": nrm(ks[10], (DEPTH, D_MODEL, D_MODEL), D_MODEL, DEEPNORM_BETA),
        "ln1_g": gain(ks[11], (DEPTH, D_MODEL)),
        "ln1_b": 0.01 * jax.random.normal(ks[12], (DEPTH, D_MODEL), f32),
        "ln2_g": gain(ks[13], (DEPTH, D_MODEL)),
        "ln2_b": 0.01 * jax.random.normal(ks[14], (DEPTH, D_MODEL), f32),
        "ffn_w1": nrm(ks[15], (n_dense, D_MODEL, D_FF), D_MODEL),
        "ffn_w3": nrm(ks[16], (n_dense, D_MODEL, D_FF), D_MODEL),
        "ffn_w2": nrm(ks[17], (n_dense, D_FF, D_MODEL), D_FF, DEEPNORM_BETA),
        "moe_router": nrm(ks[18], (n_moe, D_MODEL, N_EXPERTS), D_MODEL),
        "moe_w1": nrm(ks[19], (n_moe, N_EXPERTS, D_MODEL, D_FF_EXPERT), D_MODEL),
        "moe_w3": nrm(ks[20], (n_moe, N_EXPERTS, D_MODEL, D_FF_EXPERT), D_MODEL),
        "moe_w2": nrm(ks[21], (n_moe, N_EXPERTS, D_FF_EXPERT, D_MODEL), D_FF_EXPERT, DEEPNORM_BETA),
    }


def reference(x, positions, w_in, b_forget, g_q_a, g_kv_a, w_q_b, w_kv_b,
              w_branch_moba, w_branch_fox, w_branch_mla, w_o, ln1_g, ln1_b, ln2_g, ln2_b,
              ffn_w1, ffn_w3, ffn_w2, moe_router, moe_w1, moe_w3, moe_w2):
    cos, sin = rope_tables(positions, x.dtype)
    for l in range(DEPTH):
        h = token_mixer(x, cos, sin, w_in[l], b_forget[l], g_q_a[l], g_kv_a[l], w_q_b[l], w_kv_b[l],
                        w_branch_moba[l], w_branch_fox[l], w_branch_mla[l], w_o[l])
        x = layer_norm(DEEPNORM_ALPHA * x + h, ln1_g[l], ln1_b[l])
        if l % 2 == 0:
            f = swiglu(x, ffn_w1[l // 2], ffn_w3[l // 2], ffn_w2[l // 2])
        else:
            f = moe_swiglu(x, moe_router[l // 2], moe_w1[l // 2], moe_w3[l // 2], moe_w2[l // 2])
        x = layer_norm(DEEPNORM_ALPHA * x + f, ln2_g[l], ln2_b[l])
    return x
```

```python
import functools

import numpy as np
import jax
import jax.numpy as jnp
from jax import lax
from jax.experimental import pallas as pl
from jax.experimental.pallas import tpu as pltpu

F32 = jnp.float32
BF16 = jnp.bfloat16
NEG_INF = float("-inf")

HEAD_DIM = 128
N_HEADS = 8
MOBA_BLOCK = 256
MOBA_TOPK = 3
MLA_Q_LORA = 512
MLA_KV_LORA = 512
MLA_NOPE = 128
MLA_ROPE = 64
MLA_V = 128
MLA_QK = MLA_NOPE + MLA_ROPE
ROPE_BASE = 10000.0
N_EXPERTS = 8
MOE_ROW_BLOCK = 512
LN_EPS = 1e-5
RMS_EPS = 1e-6

LANE = 128
BF16_SUBLANES = 16
CUM_CHUNK = 256
VMEM_LIMIT = 56 * 1024 * 1024


def _params(*sem, vmem=VMEM_LIMIT):
    return pltpu.CompilerParams(dimension_semantics=sem, vmem_limit_bytes=vmem)


def _dot(a, b):
    return jnp.dot(a, b, preferred_element_type=F32)


def _dot_nt(a, b):
    return lax.dot_general(a, b, (((1,), (1,)), ((), ())), preferred_element_type=F32)


def _sigmoid(x):
    return 1.0 / (1.0 + jnp.exp(-x))


def _split3(x):
    hi = x.astype(BF16).astype(F32)
    r = x - hi
    mid = r.astype(BF16).astype(F32)
    lo = (r - mid).astype(BF16).astype(F32)
    return hi, mid, lo


def _layer_norm(y, g, b):
    mu = jnp.mean(y, axis=-1, keepdims=True)
    d = y - mu
    var = jnp.mean(d * d, axis=-1, keepdims=True)
    return d * lax.rsqrt(var + LN_EPS) * g + b


def _mm_kernel(x_ref, w_ref, o_ref, *, act):
    acc = _dot(x_ref[...], w_ref[...])
    if act == "sigmoid":
        acc = _sigmoid(acc)
    o_ref[...] = acc.astype(o_ref.dtype)


def _matmul(x, w, out_dtype, act=None, bm=1024, bn=1024):
    m, k = x.shape
    n = w.shape[1]
    bm, bn = min(bm, m), min(bn, n)
    return pl.pallas_call(
        functools.partial(_mm_kernel, act=act),
        out_shape=jax.ShapeDtypeStruct((m, n), out_dtype),
        grid=(m // bm, n // bn),
        in_specs=[pl.BlockSpec((bm, k), lambda i, j: (i, 0)),
                  pl.BlockSpec((k, bn), lambda i, j: (0, j))],
        out_specs=pl.BlockSpec((bm, bn), lambda i, j: (i, j)),
        compiler_params=_params("parallel", "arbitrary"),
        name="proj_matmul",
    )(x, w)


def _upper_tri_ones():
    r = lax.broadcasted_iota(jnp.int32, (CUM_CHUNK, CUM_CHUNK), 0)
    c = lax.broadcasted_iota(jnp.int32, (CUM_CHUNK, CUM_CHUNK), 1)
    return jnp.where(r <= c, 1.0, 0.0).astype(BF16)


def _chunk_cumsum(blk, tri, split):
    rows = blk.shape[0]
    pieces = list(_split3(blk)) if split else [blk]
    pieces.append(jnp.zeros_like(blk))
    stacked = jnp.concatenate(pieces, axis=0).astype(BF16)
    res = _dot(stacked, tri)
    out = res[0:rows]
    for p in range(1, len(pieces) - 1):
        out = out + res[p * rows:(p + 1) * rows]
    return out


def _forget_cumsum_kernel(x_ref, b_ref, o_ref):
    tri = _upper_tri_ones()
    z = x_ref[...] + b_ref[...]
    lf = jnp.minimum(z, 0.0) - jnp.log1p(jnp.exp(-jnp.abs(z)))
    carry = jnp.zeros((lf.shape[0], 1), F32)
    for c in range(lf.shape[1] // CUM_CHUNK):
        cs = _chunk_cumsum(lf[:, c * CUM_CHUNK:(c + 1) * CUM_CHUNK], tri, True) + carry
        o_ref[:, c * CUM_CHUNK:(c + 1) * CUM_CHUNK] = cs
        carry = cs[:, CUM_CHUNK - 1:CUM_CHUNK]


def _forget_cumsum(pf_t, b_forget):
    b, h, s = pf_t.shape
    return pl.pallas_call(
        _forget_cumsum_kernel,
        out_shape=jax.ShapeDtypeStruct((b, h, s), F32),
        grid=(b,),
        in_specs=[pl.BlockSpec((None, h, s), lambda i: (i, 0, 0)),
                  pl.BlockSpec((h, 1), lambda i: (0, 0))],
        out_specs=pl.BlockSpec((None, h, s), lambda i: (i, 0, 0)),
        compiler_params=_params("parallel"),
        name="forget_cumsum",
    )(pf_t, b_forget.reshape(h, 1))


def _attn_kernel(*refs, mode, tq, scale, nblk):
    if mode == "moba":
        slopes_ref, q_ref, k_ref, v_ref, o_ref, kmean_ref = refs
    elif mode == "fox":
        q_ref, k_ref, v_ref, bias_ref, o_ref = refs
    else:
        q_ref, k_ref, v_ref, o_ref = refs
    i = pl.program_id(2)
    q = q_ref[...]
    dv = v_ref.shape[-1]
    row = lax.broadcasted_iota(jnp.int32, (tq, tq), 0)
    colk = lax.broadcasted_iota(jnp.int32, (tq, tq), 1)
    causal = row >= colk

    if mode == "moba":
        h = pl.program_id(1)
        slope = slopes_ref[h]
        s_len = k_ref.shape[0]

        @pl.when(i == 0)
        def _():
            blk_of = lax.broadcasted_iota(jnp.int32, (nblk, s_len), 1) // MOBA_BLOCK
            blk_id = lax.broadcasted_iota(jnp.int32, (nblk, s_len), 0)
            ind = jnp.where(blk_of == blk_id, 1.0, 0.0).astype(BF16)
            kmean_ref[...] = _dot(ind, k_ref[...]) * (1.0 / MOBA_BLOCK)

        km_hi, km_mid, _ = _split3(kmean_ref[...])
        gate = _dot_nt(q, km_hi.astype(BF16)) + _dot_nt(q, km_mid.astype(BF16))
        colf = lax.broadcasted_iota(jnp.int32, (tq, nblk), 1).astype(F32)
        g = jnp.where(colf < i.astype(F32), gate, NEG_INF)
        sel = jnp.zeros((tq, nblk), F32)
        for _ in range(MOBA_TOPK):
            mx = jnp.max(g, axis=-1, keepdims=True)
            first = jnp.min(jnp.where(g == mx, colf, float(nblk)), axis=-1, keepdims=True)
            pick = (colf == first) & (mx > NEG_INF)
            sel = jnp.where(pick, 1.0, sel)
            g = jnp.where(pick, NEG_INF, g)
        lane_pos = lax.broadcasted_iota(jnp.int32, (1, tq), 1).astype(F32)

    def step(j, carry, diag):
        m, l, acc = carry
        start = pl.multiple_of(j * tq, tq)
        k = k_ref[pl.ds(start, tq), :]
        v = v_ref[pl.ds(start, tq), :]
        s = _dot_nt(q, k) * scale
        if mode == "moba":
            s = s + slope * (lane_pos + (j - i).astype(F32) * float(tq))
        elif mode == "fox":
            s = s - bias_ref[j]
        if diag:
            s = jnp.where(causal, s, NEG_INF)
        elif mode == "moba":
            picked = jnp.max(jnp.where(colf == j.astype(F32), sel, 0.0), axis=-1, keepdims=True)
            s = jnp.where(picked > 0.0, s, NEG_INF)
        m_new = jnp.maximum(m, jnp.max(s, axis=-1, keepdims=True))
        alpha = jnp.exp(m - m_new)
        p = jnp.exp(s - m_new)
        l = alpha * l + jnp.sum(p, axis=-1, keepdims=True)
        acc = alpha * acc + _dot(p.astype(BF16), v)
        return m_new, l, acc

    init = (jnp.full((tq, 1), NEG_INF, F32), jnp.zeros((tq, 1), F32), jnp.zeros((tq, dv), F32))
    carry = step(i, init, True)
    _, l, acc = lax.fori_loop(0, i, lambda j, c: step(j, c, False), carry)
    o_ref[...] = (acc / l).astype(o_ref.dtype)


def _attention(mode, q_arr, k_arr, v_arr, *, q_col, k_col, v_col, dqk, dv, scale, tq,
               n_heads, bias=None, slopes=None):
    b, s, _ = q_arr.shape
    nq = s // tq
    nblk = s // MOBA_BLOCK
    kern = functools.partial(_attn_kernel, mode=mode, tq=tq, scale=scale, nblk=nblk)
    out_shape = jax.ShapeDtypeStruct((b, s, n_heads * dv), BF16)
    grid = (b, n_heads, nq)
    sem = ("parallel", "parallel", "arbitrary")
    if mode == "moba":
        grid_spec = pltpu.PrefetchScalarGridSpec(
            num_scalar_prefetch=1, grid=grid,
            in_specs=[pl.BlockSpec((None, tq, dqk), lambda bi, h, i, sl: (bi, i, q_col + h)),
                      pl.BlockSpec((None, s, dqk), lambda bi, h, i, sl: (bi, 0, k_col + h)),
                      pl.BlockSpec((None, s, dv), lambda bi, h, i, sl: (bi, 0, v_col + h))],
            out_specs=pl.BlockSpec((None, tq, dv), lambda bi, h, i, sl: (bi, i, h)),
            scratch_shapes=[pltpu.VMEM((nblk, dqk), F32)])
        return pl.pallas_call(kern, out_shape=out_shape, grid_spec=grid_spec,
                              compiler_params=_params(*sem), name="moba_attention")(
                                  slopes, q_arr, k_arr, v_arr)
    in_specs = [pl.BlockSpec((None, tq, dqk), lambda bi, h, i: (bi, i, q_col + h)),
                pl.BlockSpec((None, s, dqk), lambda bi, h, i: (bi, 0, k_col + h)),
                pl.BlockSpec((None, s, dv), lambda bi, h, i: (bi, 0, v_col + h))]
    args = [q_arr, k_arr, v_arr]
    if mode == "fox":
        in_specs.append(pl.BlockSpec((None, None, nq, 1, tq), lambda bi, h, i: (bi, h, 0, 0, 0)))
        args.append(bias.reshape(b, n_heads, nq, 1, tq))
    return pl.pallas_call(kern, out_shape=out_shape, grid=grid, in_specs=in_specs,
                          out_specs=pl.BlockSpec((None, tq, dv), lambda bi, h, i: (bi, i, h)),
                          compiler_params=_params(*sem), name=mode + "_attention")(*args)


def _rope_table_kernel(pos_ref, inv_ref, cos_ref, sin_ref):
    ang = pos_ref[...].astype(F32) * inv_ref[...]
    lane = lax.broadcasted_iota(jnp.int32, ang.shape, 1)
    half = MLA_ROPE // 2
    cos_ref[...] = jnp.where(lane < MLA_ROPE, jnp.cos(ang), 0.0)
    sn = jnp.sin(ang)
    sin_ref[...] = jnp.where(lane < half, -sn, jnp.where(lane < MLA_ROPE, sn, 0.0))


def _rope_tables(positions):
    t = positions.size
    bm = min(2048, t)
    inv = ROPE_BASE ** (-np.arange(0, MLA_ROPE, 2, dtype=np.float32) / MLA_ROPE)
    inv_row = np.zeros((1, LANE), np.float32)
    inv_row[0, :MLA_ROPE // 2] = inv
    inv_row[0, MLA_ROPE // 2:MLA_ROPE] = inv
    return pl.pallas_call(
        _rope_table_kernel,
        out_shape=(jax.ShapeDtypeStruct((t, LANE), F32), jax.ShapeDtypeStruct((t, LANE), F32)),
        grid=(t // bm,),
        in_specs=[pl.BlockSpec((bm, 1), lambda i: (i, 0)), pl.BlockSpec((1, LANE), lambda i: (0, 0))],
        out_specs=(pl.BlockSpec((bm, LANE), lambda i: (i, 0)), pl.BlockSpec((bm, LANE), lambda i: (i, 0))),
        compiler_params=_params("parallel"),
        name="rope_tables",
    )(positions.reshape(t, 1), jnp.asarray(inv_row))


def _rms_to_scratch(c_ref, g_ref, xn_ref):
    x = c_ref[...]
    ms = jnp.mean(x * x, axis=-1, keepdims=True)
    xn_ref[...] = (x * lax.rsqrt(ms + RMS_EPS) * g_ref[...]).astype(BF16)


def _mla_q_kernel(c_ref, g_ref, w_ref, cos_ref, sin_ref, o_ref, xn_ref):
    @pl.when(pl.program_id(1) == 0)
    def _():
        _rms_to_scratch(c_ref, g_ref, xn_ref)

    r = _dot(xn_ref[...], w_ref[...])
    o_ref[:, :LANE] = r[:, :LANE].astype(BF16)
    roped = r[:, LANE:2 * LANE] * cos_ref[...] + r[:, 2 * LANE:3 * LANE] * sin_ref[...]
    o_ref[:, LANE:] = roped.astype(BF16)


def _mla_kv_kernel(c_ref, g_ref, w_ref, kr_ref, krs_ref, cos_ref, sin_ref, k_ref, v_ref, xn_ref):
    @pl.when(pl.program_id(1) == 0)
    def _():
        _rms_to_scratch(c_ref, g_ref, xn_ref)

    r = _dot(xn_ref[...], w_ref[...])
    k_ref[:, :LANE] = r[:, :LANE].astype(BF16)
    k_ref[:, LANE:] = (kr_ref[...] * cos_ref[...] + krs_ref[...] * sin_ref[...]).astype(BF16)
    v_ref[...] = r[:, LANE:].astype(BF16)


def _mla_q(small, g, wq, cos_t, sin_t, bm=1024):
    t = small.shape[0]
    bm = min(bm, t)
    return pl.pallas_call(
        _mla_q_kernel,
        out_shape=jax.ShapeDtypeStruct((t, N_HEADS * 2 * LANE), BF16),
        grid=(t // bm, N_HEADS),
        in_specs=[pl.BlockSpec((bm, MLA_Q_LORA), lambda i, h: (i, 0)),
                  pl.BlockSpec((1, MLA_Q_LORA), lambda i, h: (0, 0)),
                  pl.BlockSpec((MLA_Q_LORA, 3 * LANE), lambda i, h: (0, h)),
                  pl.BlockSpec((bm, LANE), lambda i, h: (i, 0)),
                  pl.BlockSpec((bm, LANE), lambda i, h: (i, 0))],
        out_specs=pl.BlockSpec((bm, 2 * LANE), lambda i, h: (i, h)),
        scratch_shapes=[pltpu.VMEM((bm, MLA_Q_LORA), BF16)],
        compiler_params=_params("parallel", "arbitrary"),
        name="mla_q_proj",
    )(small, g.reshape(1, -1), wq, cos_t, sin_t)


def _mla_kv(small, g, wkv, cos_t, sin_t, bm=1024):
    t = small.shape[0]
    bm = min(bm, t)
    kr_blk = (MLA_Q_LORA + MLA_KV_LORA) // LANE
    return pl.pallas_call(
        _mla_kv_kernel,
        out_shape=(jax.ShapeDtypeStruct((t, N_HEADS * 2 * LANE), BF16),
                   jax.ShapeDtypeStruct((t, N_HEADS * MLA_V), BF16)),
        grid=(t // bm, N_HEADS),
        in_specs=[pl.BlockSpec((bm, MLA_KV_LORA), lambda i, h: (i, 1)),
                  pl.BlockSpec((1, MLA_KV_LORA), lambda i, h: (0, 0)),
                  pl.BlockSpec((MLA_KV_LORA, 2 * LANE), lambda i, h: (0, h)),
                  pl.BlockSpec((bm, LANE), lambda i, h: (i, kr_blk)),
                  pl.BlockSpec((bm, LANE), lambda i, h: (i, kr_blk + 1)),
                  pl.BlockSpec((bm, LANE), lambda i, h: (i, 0)),
                  pl.BlockSpec((bm, LANE), lambda i, h: (i, 0))],
        out_specs=(pl.BlockSpec((bm, 2 * LANE), lambda i, h: (i, h)),
                   pl.BlockSpec((bm, MLA_V), lambda i, h: (i, h))),
        scratch_shapes=[pltpu.VMEM((bm, MLA_KV_LORA), BF16)],
        compiler_params=_params("parallel", "arbitrary"),
        name="mla_kv_proj",
    )(small, g.reshape(1, -1), wkv, small, small, cos_t, sin_t)


def _merge_kernel(oa_ref, ob_ref, oc_ref, wa_ref, wb_ref, wc_ref, ga_ref, gb_ref, gc_ref, o_ref):
    acc = ga_ref[...].astype(F32) * _dot(oa_ref[...], wa_ref[...])
    acc = acc + gb_ref[...].astype(F32) * _dot(ob_ref[...], wb_ref[...])
    acc = acc + gc_ref[...].astype(F32) * _dot(oc_ref[...], wc_ref[...])
    o_ref[...] = acc.astype(o_ref.dtype)


def _merge(oa, ob, oc, wa, wb, wc, gates, bm=1024, bn=512):
    t, kdim = oa.shape
    d = wa.shape[1]
    bm, bn = min(bm, t), min(bn, d)
    nj = d // bn
    o_spec = pl.BlockSpec((bm, kdim), lambda i, j: (i, 0))
    w_spec = pl.BlockSpec((kdim, bn), lambda i, j: (0, j))
    g_specs = [pl.BlockSpec((bm, bn), functools.partial(lambda i, j, br: (i, br * nj + j), br=br))
               for br in range(3)]
    return pl.pallas_call(
        _merge_kernel,
        out_shape=jax.ShapeDtypeStruct((t, d), BF16),
        grid=(t // bm, nj),
        in_specs=[o_spec, o_spec, o_spec, w_spec, w_spec, w_spec] + g_specs,
        out_specs=pl.BlockSpec((bm, bn), lambda i, j: (i, j)),
        compiler_params=_params("parallel", "arbitrary"),
        name="branch_merge",
    )(oa, ob, oc, wa, wb, wc, gates, gates, gates)


def _wo_ln_kernel(m_ref, w_ref, x_ref, g_ref, b_ref, o32_ref, o16_ref, *, alpha):
    y = alpha * x_ref[...] + _dot(m_ref[...], w_ref[...])
    out = _layer_norm(y, g_ref[...], b_ref[...])
    o32_ref[...] = out
    o16_ref[...] = out.astype(BF16)


def _wo_ln(merged, wo, x, g, b, alpha, bm=256):
    t, d = x.shape
    bm = min(bm, t)
    row = pl.BlockSpec((bm, d), lambda i: (i, 0))
    vec = pl.BlockSpec((1, d), lambda i: (0, 0))
    return pl.pallas_call(
        functools.partial(_wo_ln_kernel, alpha=alpha),
        out_shape=(jax.ShapeDtypeStruct((t, d), F32), jax.ShapeDtypeStruct((t, d), BF16)),
        grid=(t // bm,),
        in_specs=[row, pl.BlockSpec((d, d), lambda i: (0, 0)), row, vec, vec],
        out_specs=(row, row),
        compiler_params=_params("parallel"),
        name="out_proj_ln",
    )(merged, wo, x, g.reshape(1, d), b.reshape(1, d))


def _ffn_kernel(x16_ref, w1_ref, w3_ref, w2_ref, x32_ref, g_ref, b_ref, o32_ref, o16_ref, acc_ref,
                *, alpha):
    j = pl.program_id(1)

    @pl.when(j == 0)
    def _():
        acc_ref[...] = jnp.zeros_like(acc_ref)

    x = x16_ref[...]
    a = _dot(x, w1_ref[...])
    hmid = (a * _sigmoid(a)) * _dot(x, w3_ref[...])
    acc_ref[...] += _dot(hmid.astype(BF16), w2_ref[...])

    @pl.when(j == pl.num_programs(1) - 1)
    def _():
        out = _layer_norm(alpha * x32_ref[...] + acc_ref[...], g_ref[...], b_ref[...])
        o32_ref[...] = out
        o16_ref[...] = out.astype(BF16)


def _ffn_ln(x16, x32, w1, w3, w2, g, b, alpha, bm=512, bf=512):
    t, d = x32.shape
    f = w1.shape[1]
    bm, bf = min(bm, t), min(bf, f)
    row = pl.BlockSpec((bm, d), lambda i, j: (i, 0))
    vec = pl.BlockSpec((1, d), lambda i, j: (0, 0))
    return pl.pallas_call(
        functools.partial(_ffn_kernel, alpha=alpha),
        out_shape=(jax.ShapeDtypeStruct((t, d), F32), jax.ShapeDtypeStruct((t, d), BF16)),
        grid=(t // bm, f // bf),
        in_specs=[row, pl.BlockSpec((d, bf), lambda i, j: (0, j)), pl.BlockSpec((d, bf), lambda i, j: (0, j)),
                  pl.BlockSpec((bf, d), lambda i, j: (j, 0)), row, vec, vec],
        out_specs=(row, row),
        scratch_shapes=[pltpu.VMEM((bm, d), F32)],
        compiler_params=_params("parallel", "arbitrary"),
        name="dense_ffn_ln",
    )(x16, w1, w3, w2, x32, g.reshape(1, d), b.reshape(1, d))


def _router_kernel(x_ref, w_ref, e1_ref, e2_ref, g1_ref, g2_ref, *, n_e):
    x_hi, x_mid, _ = _split3(x_ref[...])
    w_hi, w_mid, _ = _split3(w_ref[...])
    x_hi, x_mid, w_hi, w_mid = (a.astype(BF16) for a in (x_hi, x_mid, w_hi, w_mid))
    logit = _dot_nt(w_hi, x_hi) + (_dot_nt(w_mid, x_hi) + _dot_nt(w_hi, x_mid))
    rowf = lax.broadcasted_iota(jnp.int32, logit.shape, 0).astype(F32)
    logit = jnp.where(rowf < float(n_e), logit, NEG_INF)
    m1 = jnp.max(logit, axis=0, keepdims=True)
    i1 = jnp.min(jnp.where(logit == m1, rowf, float(n_e)), axis=0, keepdims=True)
    rest = jnp.where(rowf == i1, NEG_INF, logit)
    m2 = jnp.max(rest, axis=0, keepdims=True)
    i2 = jnp.min(jnp.where(rest == m2, rowf, float(n_e)), axis=0, keepdims=True)
    ex = jnp.exp(m2 - m1)
    e1_ref[...] = i1.astype(jnp.int32)
    e2_ref[...] = i2.astype(jnp.int32)
    g1_ref[...] = 1.0 / (1.0 + ex)
    g2_ref[...] = ex / (1.0 + ex)


def _router(x32, w_router, bm=256):
    t, d = x32.shape
    bm = min(bm, t)
    n_e = w_router.shape[1]
    n_pad = -(-n_e // BF16_SUBLANES) * BF16_SUBLANES
    w_router_t = jnp.pad(w_router.T, ((0, n_pad - n_e), (0, 0)))
    rowspec = pl.BlockSpec((1, bm), lambda i: (0, i))
    return pl.pallas_call(
        functools.partial(_router_kernel, n_e=n_e),
        out_shape=(jax.ShapeDtypeStruct((1, t), jnp.int32), jax.ShapeDtypeStruct((1, t), jnp.int32),
                   jax.ShapeDtypeStruct((1, t), F32), jax.ShapeDtypeStruct((1, t), F32)),
        grid=(t // bm,),
        in_specs=[pl.BlockSpec((bm, d), lambda i: (i, 0)), pl.BlockSpec((n_pad, d), lambda i: (0, 0))],
        out_specs=(rowspec, rowspec, rowspec, rowspec),
        compiler_params=_params("parallel"),
        name="moe_router",
    )(x32, w_router_t)


def _dispatch_index_kernel(e1_ref, e2_ref, d1_ref, d2_ref, blk_ref, nact_ref, *, n_e, row_block):
    t = e1_ref.shape[1]
    tri = _upper_tri_ones()
    rowi = lax.broadcasted_iota(jnp.int32, (n_e, t), 0)
    oh1 = jnp.where(rowi == e1_ref[...], 1.0, 0.0)
    oh2 = jnp.where(rowi == e2_ref[...], 1.0, 0.0)
    both = oh1 + oh2
    carry = jnp.zeros((n_e, 1), F32)
    chunks = []
    for c in range(t // CUM_CHUNK):
        blk = both[:, c * CUM_CHUNK:(c + 1) * CUM_CHUNK]
        cs = _chunk_cumsum(blk, tri, False) + carry
        chunks.append(cs - blk)
        carry = cs[:, CUM_CHUNK - 1:CUM_CHUNK]
    rank = jnp.concatenate(chunks, axis=1)
    counts = carry
    padded = jnp.floor((counts + float(row_block - 1)) * (1.0 / row_block)) * float(row_block)
    starts = []
    run = jnp.zeros((1, 1), F32)
    for e in range(n_e):
        starts.append(run)
        run = run + padded[e:e + 1]
    start = jnp.concatenate(starts, axis=0)
    end = start + padded
    slot = start + rank
    d1_ref[...] = jnp.sum(oh1 * slot, axis=0, keepdims=True).astype(jnp.int32)
    d2_ref[...] = jnp.sum(oh2 * slot, axis=0, keepdims=True).astype(jnp.int32)
    blk_start = lax.broadcasted_iota(jnp.int32, (n_e, LANE), 1).astype(F32) * float(row_block)
    owner = jnp.sum(jnp.where(end <= blk_start, 1.0, 0.0), axis=0, keepdims=True)
    blk_ref[...] = jnp.minimum(owner, float(n_e - 1)).astype(jnp.int32)
    nact_ref[...] = jnp.broadcast_to(run * (1.0 / row_block), (1, LANE)).astype(jnp.int32)


def _dispatch_index(e1, e2, n_e, row_block):
    t = e1.shape[1]
    return pl.pallas_call(
        functools.partial(_dispatch_index_kernel, n_e=n_e, row_block=row_block),
        out_shape=(jax.ShapeDtypeStruct((1, t), jnp.int32), jax.ShapeDtypeStruct((1, t), jnp.int32),
                   jax.ShapeDtypeStruct((1, LANE), jnp.int32), jax.ShapeDtypeStruct((1, LANE), jnp.int32)),
        name="moe_dispatch_index",
    )(e1, e2)


def _scatter_rows_kernel(d1_ref, d2_ref, x_ref, init_ref, xb_ref, sem, *, tm):
    del init_ref
    base = pl.program_id(0) * tm

    def issue(r, _):
        pltpu.make_async_copy(x_ref.at[r], xb_ref.at[d1_ref[base + r]], sem).start()
        pltpu.make_async_copy(x_ref.at[r], xb_ref.at[d2_ref[base + r]], sem).start()
        return 0

    lax.fori_loop(0, tm, issue, 0)

    def drain(r, _):
        pltpu.make_async_copy(x_ref.at[0], xb_ref.at[0], sem).wait()
        pltpu.make_async_copy(x_ref.at[0], xb_ref.at[0], sem).wait()
        return 0

    lax.fori_loop(0, tm, drain, 0)


def _scatter_rows(x16, d1, d2, n_rows, tm=512):
    t, d = x16.shape
    tm = min(tm, t)
    sub = d // LANE
    x3 = x16.reshape(t, sub, LANE)
    init = jnp.zeros((n_rows, sub, LANE), BF16)
    grid_spec = pltpu.PrefetchScalarGridSpec(
        num_scalar_prefetch=2, grid=(t // tm,),
        in_specs=[pl.BlockSpec((tm, sub, LANE), lambda i, a, b: (i, 0, 0)),
                  pl.BlockSpec(memory_space=pl.ANY)],
        out_specs=pl.BlockSpec(memory_space=pl.ANY),
        scratch_shapes=[pltpu.SemaphoreType.DMA(())])
    xb = pl.pallas_call(
        functools.partial(_scatter_rows_kernel, tm=tm),
        out_shape=jax.ShapeDtypeStruct((n_rows, sub, LANE), BF16),
        grid_spec=grid_spec,
        input_output_aliases={3: 0},
        compiler_params=_params("arbitrary"),
        name="moe_scatter_rows",
    )(d1, d2, x3, init)
    return xb.reshape(n_rows, d)


def _expert_up_kernel(blk_ref, nact_ref, x_ref, w1_ref, w3_ref, h_ref):
    active = pl.program_id(1) < nact_ref[0]

    @pl.when(active)
    def _():
        x = x_ref[...]
        a = _dot(x, w1_ref[...])
        h_ref[...] = ((a * _sigmoid(a)) * _dot(x, w3_ref[...])).astype(BF16)

    @pl.when(jnp.logical_not(active))
    def _():
        h_ref[...] = jnp.zeros_like(h_ref)


def _expert_down_kernel(blk_ref, nact_ref, h_ref, w2_ref, y_ref):
    active = pl.program_id(1) < nact_ref[0]

    @pl.when(active)
    def _():
        y_ref[...] = _dot(h_ref[...], w2_ref[...])

    @pl.when(jnp.logical_not(active))
    def _():
        y_ref[...] = jnp.zeros_like(y_ref)


def _experts(xb, w1, w3, w2, blk_e, nact, row_block, bf=1024, bn=1024):
    n_rows, d = xb.shape
    f = w1.shape[2]
    n_blk = n_rows // row_block
    bf, bn = min(bf, f), min(bn, d)

    def rows(j, i, blk, na):
        return (jnp.minimum(i, na[0] - 1), 0)

    up_spec = pltpu.PrefetchScalarGridSpec(
        num_scalar_prefetch=2, grid=(f // bf, n_blk),
        in_specs=[pl.BlockSpec((row_block, d), rows),
                  pl.BlockSpec((None, d, bf), lambda j, i, blk, na: (blk[i], 0, j)),
                  pl.BlockSpec((None, d, bf), lambda j, i, blk, na: (blk[i], 0, j))],
        out_specs=pl.BlockSpec((row_block, bf), lambda j, i, blk, na: (i, j)))
    hmid = pl.pallas_call(
        _expert_up_kernel, out_shape=jax.ShapeDtypeStruct((n_rows, f), BF16), grid_spec=up_spec,
        compiler_params=_params("arbitrary", "arbitrary"), name="moe_expert_up",
    )(blk_e, nact, xb, w1, w3)
    down_spec = pltpu.PrefetchScalarGridSpec(
        num_scalar_prefetch=2, grid=(d // bn, n_blk),
        in_specs=[pl.BlockSpec((row_block, f), rows),
                  pl.BlockSpec((None, f, bn), lambda j, i, blk, na: (blk[i], 0, j))],
        out_specs=pl.BlockSpec((row_block, bn), lambda j, i, blk, na: (i, j)))
    return pl.pallas_call(
        _expert_down_kernel, out_shape=jax.ShapeDtypeStruct((n_rows, d), F32), grid_spec=down_spec,
        compiler_params=_params("arbitrary", "arbitrary"), name="moe_expert_down",
    )(blk_e, nact, hmid, w2)


def _combine_ln_kernel(d1_ref, d2_ref, yb_ref, x_ref, g1_ref, g2_ref, g_ref, b_ref, o32_ref, o16_ref,
                       buf1, buf2, sem, *, tm, alpha):
    base = pl.program_id(0) * tm

    def issue(r, _):
        pltpu.make_async_copy(yb_ref.at[pl.ds(d1_ref[base + r], 1)], buf1.at[pl.ds(r, 1)], sem).start()
        pltpu.make_async_copy(yb_ref.at[pl.ds(d2_ref[base + r], 1)], buf2.at[pl.ds(r, 1)], sem).start()
        return 0

    lax.fori_loop(0, tm, issue, 0)

    def drain(r, _):
        pltpu.make_async_copy(yb_ref.at[pl.ds(0, 1)], buf1.at[pl.ds(0, 1)], sem).wait()
        pltpu.make_async_copy(yb_ref.at[pl.ds(0, 1)], buf2.at[pl.ds(0, 1)], sem).wait()
        return 0

    lax.fori_loop(0, tm, drain, 0)
    f = g1_ref[...] * buf1[...] + g2_ref[...] * buf2[...]
    out = _layer_norm(alpha * x_ref[...] + f, g_ref[...], b_ref[...])
    o32_ref[...] = out
    o16_ref[...] = out.astype(BF16)


def _combine_ln(yb, d1, d2, x32, g1c, g2c, g, b, alpha, tm=256):
    t, d = x32.shape
    tm = min(tm, t)
    row = pl.BlockSpec((tm, d), lambda i, a, c: (i, 0))
    col = pl.BlockSpec((tm, 1), lambda i, a, c: (i, 0))
    vec = pl.BlockSpec((1, d), lambda i, a, c: (0, 0))
    grid_spec = pltpu.PrefetchScalarGridSpec(
        num_scalar_prefetch=2, grid=(t // tm,),
        in_specs=[pl.BlockSpec(memory_space=pl.ANY), row, col, col, vec, vec],
        out_specs=(row, row),
        scratch_shapes=[pltpu.VMEM((tm, d), F32), pltpu.VMEM((tm, d), F32), pltpu.SemaphoreType.DMA(())])
    return pl.pallas_call(
        functools.partial(_combine_ln_kernel, tm=tm, alpha=alpha),
        out_shape=(jax.ShapeDtypeStruct((t, d), F32), jax.ShapeDtypeStruct((t, d), BF16)),
        grid_spec=grid_spec,
        compiler_params=_params("arbitrary"),
        name="moe_combine_ln",
    )(d1, d2, yb, x32, g1c, g2c, g.reshape(1, d), b.reshape(1, d))


def _moe_ln(x16, x32, w_router, w1, w3, w2, g, b, alpha):
    t, d = x32.shape
    n_e = w_router.shape[1]
    n_rows = (-(-(t * 2) // MOE_ROW_BLOCK) + n_e) * MOE_ROW_BLOCK
    e1, e2, g1, g2 = _router(x32, w_router)
    d1, d2, blk_e, nact = _dispatch_index(e1, e2, n_e, MOE_ROW_BLOCK)
    d1, d2 = d1.reshape(t), d2.reshape(t)
    blk_e = blk_e.reshape(LANE)[:n_rows // MOE_ROW_BLOCK]
    nact = nact.reshape(LANE)[:1]
    xb = _scatter_rows(x16, d1, d2, n_rows)
    yb = _experts(xb, w1, w3, w2, blk_e, nact, MOE_ROW_BLOCK)
    return _combine_ln(yb, d1, d2, x32, g1.reshape(t, 1), g2.reshape(t, 1), g, b, alpha)


def _swap_halves(w):
    half = w.shape[-1] // 2
    return jnp.concatenate([w[..., half:], w[..., :half]], axis=-1)


def _prep_w_in(w_in, d_model):
    w_attn = N_HEADS * HEAD_DIM * 3
    o_forget = 2 * w_attn
    o_cq = o_forget + N_HEADS
    o_ckv = o_cq + MLA_Q_LORA
    o_kr = o_ckv + MLA_KV_LORA
    o_gate = o_kr + MLA_ROPE
    kr = w_in[:, o_kr:o_gate]
    zpad = jnp.zeros((w_in.shape[0], LANE - MLA_ROPE), w_in.dtype)
    fpad = jnp.zeros((w_in.shape[0], LANE - N_HEADS), w_in.dtype)
    small = jnp.concatenate([w_in[:, o_cq:o_kr], kr, zpad, _swap_halves(kr), zpad,
                             w_in[:, o_forget:o_cq], fpad], axis=1)
    return (w_in[:, :o_forget].astype(BF16), small.astype(BF16), w_in[:, o_gate:].astype(BF16))


def _prep_w_q_b(w_q_b):
    w = w_q_b.reshape(MLA_Q_LORA, N_HEADS, MLA_QK)
    nope, pe = w[..., :MLA_NOPE], w[..., MLA_NOPE:]
    z = jnp.zeros((MLA_Q_LORA, N_HEADS, LANE - MLA_ROPE), w.dtype)
    return jnp.concatenate([nope, pe, z, _swap_halves(pe), z], axis=-1).reshape(
        MLA_Q_LORA, N_HEADS * 3 * LANE).astype(BF16)


def kernel(x, positions, w_in, b_forget, g_q_a, g_kv_a, w_q_b, w_kv_b, w_branch_moba, w_branch_fox,
           w_branch_mla, w_o, ln1_g, ln1_b, ln2_g, ln2_b, ffn_w1, ffn_w3, ffn_w2, moe_router, moe_w1,
           moe_w3, moe_w2):
    b, s, d = x.shape
    t = b * s
    depth = w_in.shape[0]
    alpha = (2 * depth) ** 0.25
    cos_t, sin_t = _rope_tables(positions)
    slopes = jnp.asarray(2.0 ** (-8.0 * np.arange(1, N_HEADS + 1, dtype=np.float32) / N_HEADS))
    n_attn = N_HEADS * HEAD_DIM
    x32 = x.reshape(t, d)
    x16 = x32.astype(BF16)
    for l in range(depth):
        w_qkv, w_small, w_gate = _prep_w_in(w_in[l], d)
        qkv = _matmul(x16, w_qkv, BF16).reshape(b, s, -1)
        small = _matmul(x16, w_small, F32, bn=w_small.shape[1])
        gates = _matmul(x16, w_gate, BF16, act="sigmoid")

        o_moba = _attention("moba", qkv, qkv, qkv, q_col=0, k_col=N_HEADS, v_col=2 * N_HEADS,
                            dqk=HEAD_DIM, dv=HEAD_DIM, scale=HEAD_DIM ** -0.5, tq=MOBA_BLOCK,
                            n_heads=N_HEADS, slopes=slopes)

        f_off = MLA_Q_LORA + MLA_KV_LORA + 2 * LANE
        pf_t = small[:, f_off:f_off + N_HEADS].reshape(b, s, N_HEADS).transpose(0, 2, 1)
        c_decay = _forget_cumsum(pf_t, b_forget[l])
        o_fox = _attention("fox", qkv, qkv, qkv, q_col=3 * N_HEADS, k_col=4 * N_HEADS, v_col=5 * N_HEADS,
                           dqk=HEAD_DIM, dv=HEAD_DIM, scale=HEAD_DIM ** -0.5, tq=min(512, s),
                           n_heads=N_HEADS, bias=c_decay)

        q_m = _mla_q(small, g_q_a[l], _prep_w_q_b(w_q_b[l]), cos_t, sin_t).reshape(b, s, -1)
        k_m, v_m = _mla_kv(small, g_kv_a[l], w_kv_b[l].astype(BF16), cos_t, sin_t)
        o_mla = _attention("mla", q_m, k_m.reshape(b, s, -1), v_m.reshape(b, s, -1), q_col=0, k_col=0,
                           v_col=0, dqk=2 * LANE, dv=MLA_V, scale=MLA_QK ** -0.5, tq=min(512, s),
                           n_heads=N_HEADS)

        merged = _merge(o_moba.reshape(t, n_attn), o_fox.reshape(t, n_attn), o_mla.reshape(t, n_attn),
                        w_branch_moba[l].astype(BF16), w_branch_fox[l].astype(BF16),
                        w_branch_mla[l].astype(BF16), gates)
        x32, x16 = _wo_ln(merged, w_o[l].astype(BF16), x32, ln1_g[l], ln1_b[l], alpha)
        if l % 2 == 0:
            x32, x16 = _ffn_ln(x16, x32, ffn_w1[l // 2].astype(BF16), ffn_w3[l // 2].astype(BF16),
                               ffn_w2[l // 2].astype(BF16), ln2_g[l], ln2_b[l], alpha)
        else:
            x32, x16 = _moe_ln(x16, x32, moe_router[l // 2], moe_w1[l // 2].astype(BF16),
                               moe_w3[l // 2].astype(BF16), moe_w2[l // 2].astype(BF16),
                               ln2_g[l], ln2_b[l], alpha)
    return x32.reshape(b, s, d)
```
